```python
import math
import jax, jax.numpy as jnp
from jax import lax
import numpy as np

D_MODEL = 1024
BATCH = 16
SEQ = 2048
DEPTH = 2
DEC_BATCH = 8
DEC_SEQ = 2048
PAST_LEN = 128

N_GROUPS = 4
GROUP_W = D_MODEL // N_GROUPS
HEAD_DIM = 64
CONV_A_W = 3
MLA_HEADS = GROUP_W // HEAD_DIM
MLA_NOPE = 64
MLA_ROPE = 32
MLA_V = GROUP_W // MLA_HEADS
MLA_Q_RANK = 3 * D_MODEL // 16
MLA_KV_RANK = D_MODEL // 8
ROPE_THETA = 10000.0
CONV_C_W = 31
SWA_HEADS = GROUP_W // HEAD_DIM
SWA_KV_HEADS = 2
SWA_G = SWA_HEADS // SWA_KV_HEADS
WINDOW = 128
BLOCK = 128
N_BUCKETS = 32
MAX_DIST = 128
EPS = 1e-6
NEG = -1e30

IN_SIZES = (
    GROUP_W, GROUP_W, GROUP_W, GROUP_W,
    MLA_Q_RANK, MLA_KV_RANK, MLA_ROPE, GROUP_W,
    GROUP_W, GROUP_W, GROUP_W,
    SWA_HEADS * HEAD_DIM, SWA_KV_HEADS * HEAD_DIM,
    SWA_KV_HEADS * HEAD_DIM, GROUP_W,
)
IN_WIDTH = sum(IN_SIZES)

kernel_name = "hybrid_parallel_group_encoder"


def _split_points():
    return [int(v) for v in np.cumsum(IN_SIZES)[:-1]]


def _rmsnorm(x, g):
    xf = x.astype(jnp.float32)
    y = xf * lax.rsqrt(jnp.mean(xf * xf, axis=-1, keepdims=True) + EPS)
    return (y * g.astype(jnp.float32)).astype(x.dtype)


def _layernorm(x, g, b):
    xf = x.astype(jnp.float32)
    mu = jnp.mean(xf, axis=-1, keepdims=True)
    xc = xf - mu
    y = xc * lax.rsqrt(jnp.mean(xc * xc, axis=-1, keepdims=True) + EPS)
    return (y * g.astype(jnp.float32) + b.astype(jnp.float32)).astype(x.dtype)


def _depthwise_conv(x, w, b):
    k = w.shape[0]
    p = (k - 1) // 2
    y = lax.conv_general_dilated(
        x, w[:, None, :].astype(x.dtype), window_strides=(1,), padding=[(p, p)],
        dimension_numbers=("NWC", "WIO", "NWC"), feature_group_count=x.shape[-1])
    return y + b.astype(x.dtype)


def _rope_tables(s):
    inv = ROPE_THETA ** (-jnp.arange(0, MLA_ROPE, 2, dtype=jnp.float32) / MLA_ROPE)
    ang = jnp.arange(s, dtype=jnp.float32)[:, None] * inv[None, :]
    return jnp.cos(ang), jnp.sin(ang)


def _rope(t, cos, sin):
    c = cos[:, None, :].astype(t.dtype)
    s = sin[:, None, :].astype(t.dtype)
    t1, t2 = jnp.split(t, 2, axis=-1)
    return jnp.concatenate([t1 * c - t2 * s, t1 * s + t2 * c], axis=-1)


def _t5_buckets(rel):
    half = N_BUCKETS // 2
    max_exact = half // 2
    n = np.abs(rel)
    large = max_exact + (np.log(np.maximum(n, 1) / max_exact) / math.log(MAX_DIST / max_exact)
                         * (half - max_exact)).astype(np.int32)
    large = np.minimum(large, half - 1)
    return (rel > 0).astype(np.int32) * half + np.where(n < max_exact, n, large).astype(np.int32)


def _short_conv_mixer(bg, cg, hv, w, b):
    return bg * _depthwise_conv(cg * hv, w, b)


def _mla_mixer(q_lat, kv_lat, k_rope, g_q, w_q_up, g_kv, w_kv_up, cos, sin):
    bsz, s, _ = q_lat.shape
    nb = s // BLOCK
    q = (_rmsnorm(q_lat, g_q) @ w_q_up).reshape(bsz, s, MLA_HEADS, MLA_NOPE + MLA_ROPE)
    q_nope, q_rot = q[..., :MLA_NOPE], _rope(q[..., MLA_NOPE:], cos, sin)
    kv = (_rmsnorm(kv_lat, g_kv) @ w_kv_up).reshape(bsz, s, MLA_HEADS, MLA_NOPE + MLA_V)
    k_nope, v = kv[..., :MLA_NOPE], kv[..., MLA_NOPE:]
    k_rot = _rope(k_rope[:, :, None, :], cos, sin)[:, :, 0, :]
    scale = (MLA_NOPE + MLA_ROPE) ** -0.5
    qn = q_nope.reshape(bsz, nb, BLOCK, MLA_HEADS, MLA_NOPE).transpose(1, 0, 2, 3, 4)
    qr = q_rot.reshape(bsz, nb, BLOCK, MLA_HEADS, MLA_ROPE).transpose(1, 0, 2, 3, 4)

    def attend(args):
        qn_b, qr_b = args
        sc = (jnp.einsum('bqhd,bkhd->bhqk', qn_b, k_nope)
              + jnp.einsum('bqhr,bkr->bhqk', qr_b, k_rot)).astype(jnp.float32) * scale
        p = jax.nn.softmax(sc, axis=-1).astype(v.dtype)
        return jnp.einsum('bhqk,bkhd->bqhd', p, v)

    o = lax.map(attend, (qn, qr))
    return o.transpose(1, 0, 2, 3, 4).reshape(bsz, s, MLA_HEADS * MLA_V)


def _conformer_conv_mixer(va, vb, w_dw, b_dw, ln_g, ln_b, w_pw):
    u = va * jax.nn.sigmoid(vb)
    u = _depthwise_conv(u, w_dw, b_dw)
    u = jax.nn.silu(_layernorm(u, ln_g, ln_b))
    return u @ w_pw


def _windowed_gqa(q, k, v, sink, rel_bias):
    bsz, s, _ = q.shape
    nb = s // BLOCK
    q = q.reshape(bsz, nb, BLOCK, SWA_KV_HEADS, SWA_G, HEAD_DIM)

    def band(t):
        t = t.reshape(bsz, s, SWA_KV_HEADS, HEAD_DIM)
        t = jnp.pad(t, ((0, 0), (BLOCK, BLOCK), (0, 0), (0, 0)))
        t = t.reshape(bsz, nb + 2, BLOCK, SWA_KV_HEADS, HEAD_DIM)
        return jnp.concatenate([t[:, :-2], t[:, 1:-1], t[:, 2:]], axis=2)

    kb, vb = band(k), band(v)
    sc = jnp.einsum('bnqkgd,bnjkd->bnkgqj', q, kb).astype(jnp.float32) * (HEAD_DIM ** -0.5)
    rel = np.arange(3 * BLOCK)[None, :] - BLOCK - np.arange(BLOCK)[:, None]
    bias = rel_bias.astype(jnp.float32)[_t5_buckets(rel)]
    bias = bias.transpose(2, 0, 1).reshape(SWA_KV_HEADS, SWA_G, BLOCK, 3 * BLOCK)
    key_pos = np.arange(nb)[:, None] * BLOCK - BLOCK + np.arange(3 * BLOCK)[None, :]
    valid = (np.abs(rel) <= WINDOW)[None] & ((key_pos >= 0) & (key_pos < s))[:, None, :]
    sc = jnp.where(valid[None, :, None, None], sc + bias, NEG)
    sink_l = sink.astype(jnp.float32).reshape(1, 1, SWA_KV_HEADS, SWA_G, 1, 1)
    m = jnp.maximum(jnp.max(sc, axis=-1, keepdims=True), sink_l)
    e = jnp.exp(sc - m)
    p = e / (jnp.sum(e, axis=-1, keepdims=True) + jnp.exp(sink_l - m))
    o = jnp.einsum('bnkgqj,bnjkd->bnqkgd', p.astype(vb.dtype), vb)
    return o.reshape(bsz, s, SWA_HEADS * HEAD_DIM)


def _layer(x, g_pre, g_post, w_in, w_out, conv_a_w, conv_a_b, mla_g_q, mla_w_q_up,
           mla_g_kv, mla_w_kv_up, conf_w_dw, conf_b_dw, conf_ln_g, conf_ln_b, conf_w_pw,
           swa_sink, rel_bias, cos, sin):
    h = _rmsnorm(x, g_pre)
    proj = h @ w_in
    (a_b, a_c, a_h, a_g, b_ql, b_kvl, b_kr, b_g, c_va, c_vb, c_g,
     d_q, d_k, d_v, d_g) = jnp.split(proj, _split_points(), axis=-1)
    ya = _short_conv_mixer(a_b, a_c, a_h, conv_a_w, conv_a_b) * jax.nn.silu(a_g)
    yb = _mla_mixer(b_ql, b_kvl, b_kr, mla_g_q, mla_w_q_up, mla_g_kv, mla_w_kv_up,
                    cos, sin) * jax.nn.silu(b_g)
    yc = _conformer_conv_mixer(c_va, c_vb, conf_w_dw, conf_b_dw, conf_ln_g, conf_ln_b,
                               conf_w_pw) * jax.nn.silu(c_g)
    yd = _windowed_gqa(d_q, d_k, d_v, swa_sink, rel_bias) * jax.nn.silu(d_g)
    out = jnp.concatenate([ya, yb, yc, yd], axis=-1) @ w_out
    return x + _rmsnorm(out, g_post)


def _trunk(x, g_pre, g_post, w_in, w_out, conv_a_w, conv_a_b, mla_g_q, mla_w_q_up,
           mla_g_kv, mla_w_kv_up, conf_w_dw, conf_b_dw, conf_ln_g, conf_ln_b, conf_w_pw,
           swa_sink, rel_bias):
    cos, sin = _rope_tables(x.shape[1])
    for l in range(DEPTH):
        x = _layer(x, g_pre[l], g_post[l], w_in[l], w_out[l], conv_a_w[l], conv_a_b[l],
                   mla_g_q[l], mla_w_q_up[l], mla_g_kv[l], mla_w_kv_up[l],
                   conf_w_dw[l], conf_b_dw[l], conf_ln_g[l], conf_ln_b[l], conf_w_pw[l],
                   swa_sink[l], rel_bias, cos, sin)
    return x


def setup_inputs(seed: int = 0) -> dict:
    key = jax.random.key(seed)
    ks = jax.random.split(key, 20)
    f = jnp.float32
    nrm = lambda k, shape, sc: jax.random.normal(k, shape, f) * sc
    return {
        "x_prompt": jax.random.normal(ks[0], (BATCH, SEQ, D_MODEL), f),
        "x_sample": jax.random.normal(ks[1], (DEC_BATCH, DEC_SEQ, D_MODEL), f),
        "g_pre": 1.0 + nrm(ks[2], (DEPTH, D_MODEL), 0.01),
        "g_post": 1.0 + nrm(ks[3], (DEPTH, D_MODEL), 0.01),
        "w_in": nrm(ks[4], (DEPTH, D_MODEL, IN_WIDTH), D_MODEL ** -0.5),
        "w_out": nrm(ks[5], (DEPTH, N_GROUPS * GROUP_W, D_MODEL), (N_GROUPS * GROUP_W) ** -0.5),
        "conv_a_w": nrm(ks[6], (DEPTH, CONV_A_W, GROUP_W), CONV_A_W ** -0.5),
        "conv_a_b": nrm(ks[7], (DEPTH, GROUP_W), 0.01),
        "mla_g_q": 1.0 + nrm(ks[8], (DEPTH, MLA_Q_RANK), 0.01),
        "mla_w_q_up": nrm(ks[9], (DEPTH, MLA_Q_RANK, MLA_HEADS * (MLA_NOPE + MLA_ROPE)), MLA_Q_RANK ** -0.5),
        "mla_g_kv": 1.0 + nrm(ks[10], (DEPTH, MLA_KV_RANK), 0.01),
        "mla_w_kv_up": nrm(ks[11], (DEPTH, MLA_KV_RANK, MLA_HEADS * (MLA_NOPE + MLA_V)), MLA_KV_RANK ** -0.5),
        "conf_w_dw": nrm(ks[12], (DEPTH, CONV_C_W, GROUP_W), CONV_C_W ** -0.5),
        "conf_b_dw": nrm(ks[13], (DEPTH, GROUP_W), 0.01),
        "conf_ln_g": 1.0 + nrm(ks[14], (DEPTH, GROUP_W), 0.01),
        "conf_ln_b": nrm(ks[15], (DEPTH, GROUP_W), 0.01),
        "conf_w_pw": nrm(ks[16], (DEPTH, GROUP_W, GROUP_W), GROUP_W ** -0.5),
        "swa_sink": nrm(ks[17], (DEPTH, SWA_HEADS), 0.5),
        "rel_bias": nrm(ks[18], (N_BUCKETS, SWA_HEADS), 0.5),
    }


def reference(x_prompt, x_sample, g_pre, g_post, w_in, w_out, conv_a_w, conv_a_b,
              mla_g_q, mla_w_q_up, mla_g_kv, mla_w_kv_up, conf_w_dw, conf_b_dw,
              conf_ln_g, conf_ln_b, conf_w_pw, swa_sink, rel_bias):
    y_prompt = _trunk(x_prompt, g_pre, g_post, w_in, w_out, conv_a_w, conv_a_b, mla_g_q,
                      mla_w_q_up, mla_g_kv, mla_w_kv_up, conf_w_dw, conf_b_dw, conf_ln_g,
                      conf_ln_b, conf_w_pw, swa_sink, rel_bias)
    y_sample = _trunk(x_sample, g_pre, g_post, w_in, w_out, conv_a_w, conv_a_b, mla_g_q,
                      mla_w_q_up, mla_g_kv, mla_w_kv_up, conf_w_dw, conf_b_dw, conf_ln_g,
                      conf_ln_b, conf_w_pw, swa_sink, rel_bias)
    return (y_prompt, y_sample)
```

```python
import functools
import math

import jax
import jax.numpy as jnp
import numpy as np
from jax import lax
from jax.experimental import pallas as pl
from jax.experimental.pallas import tpu as pltpu

F32 = jnp.float32
BF16 = jnp.bfloat16

D_MODEL = 1024
GROUP_W = 256
HEAD_DIM = 64
N_HEADS = 4
CONV_A_W = 3
CONV_C_W = 31
MLA_NOPE = 64
MLA_ROPE = 32
MLA_Q_RANK = 192
MLA_KV_RANK = 128
ROPE_THETA = 10000.0
SWA_KV_HEADS = 2
WINDOW = 128
BLOCK = 128
N_BUCKETS = 32
MAX_DIST = 128
EPS = 1e-6
NEG = -1e30

WA = 4 * GROUP_W
WB = GROUP_W + MLA_KV_RANK + 256
WC = 3 * GROUP_W
WD = 3 * GROUP_W
W_IN_PAD = WA + WB + WC + WD

VMEM_LIMIT_BYTES = 56 * 1024 * 1024
CONV_A_PAD = 8
CONV_C_PAD = 16


def _silu(x):
    return x * jax.nn.sigmoid(x)


def _dot(a, b):
    return jnp.dot(a, b, preferred_element_type=F32)


def _dot_nt(a, b):
    return lax.dot_general(a, b, (((1,), (1,)), ((), ())), preferred_element_type=F32)


def _params(n_grid_dims):
    return pltpu.CompilerParams(
        dimension_semantics=("parallel",) * n_grid_dims,
        vmem_limit_bytes=VMEM_LIMIT_BYTES)


def _full(shape):
    return pl.BlockSpec(shape, lambda *_: (0,) * len(shape))


def _in_proj_kernel(x_ref, g_ref, w_ref, pa_ref, pb_ref, pc_ref, pd_ref):
    x = x_ref[...]
    h = x * lax.rsqrt(jnp.mean(x * x, axis=-1, keepdims=True) + EPS) * g_ref[...]
    h = h.astype(BF16)
    off = 0
    for ref, width in ((pa_ref, WA), (pb_ref, WB), (pc_ref, WC), (pd_ref, WD)):
        ref[...] = _dot(h, w_ref[:, off:off + width]).astype(BF16)
        off += width


def _in_proj(x2d, g_pre, w_in, tm):
    t = x2d.shape[0]
    row = lambda width: pl.BlockSpec((tm, width), lambda i: (i, 0))
    return pl.pallas_call(
        _in_proj_kernel,
        out_shape=[jax.ShapeDtypeStruct((t, w), BF16) for w in (WA, WB, WC, WD)],
        grid=(t // tm,),
        in_specs=[row(D_MODEL), _full((1, D_MODEL)), _full((D_MODEL, W_IN_PAD))],
        out_specs=[row(WA), row(WB), row(WC), row(WD)],
        compiler_params=_params(1),
        name="in_proj",
    )(x2d, g_pre, w_in)


SUBLANES = 8


def _shifted_conv(scr, w_ref, r, chunk, width, pad):
    c0 = pad - (width - 1) // 2
    rows = chunk + SUBLANES
    n_a = (c0 + width - 1) // SUBLANES + 1
    tiles = [scr[pl.ds(r + SUBLANES * a, rows), :] for a in range(n_a)]
    acc = None
    for b in range(SUBLANES):
        z = None
        for k in range(width):
            if (k + c0) % SUBLANES != b:
                continue
            term = tiles[(k + c0) // SUBLANES] * w_ref[k:k + 1, :]
            z = term if z is None else z + term
        if z is None:
            continue
        z = z[b:b + chunk]
        acc = z if acc is None else acc + z
    return acc


def _conv_kernel(pa_ref, pc_ref, wa_ref, ba_ref, wc_ref, bc_ref, lng_ref, lnb_ref, wpw_ref,
                 ya_ref, yc_ref, ua_scr, uc_scr, *, seq, chunk):
    n_chunks = seq // chunk
    ua_scr[0:CONV_A_PAD, :] = jnp.zeros((CONV_A_PAD, GROUP_W), F32)
    ua_scr[CONV_A_PAD + seq:, :] = jnp.zeros((CONV_A_PAD, GROUP_W), F32)
    uc_scr[0:CONV_C_PAD, :] = jnp.zeros((CONV_C_PAD, GROUP_W), F32)
    uc_scr[CONV_C_PAD + seq:, :] = jnp.zeros((CONV_C_PAD, GROUP_W), F32)

    def fill(i, carry):
        r = pl.multiple_of(i * chunk, chunk)
        a_c = pa_ref[0, pl.ds(r, chunk), GROUP_W:2 * GROUP_W].astype(F32)
        a_h = pa_ref[0, pl.ds(r, chunk), 2 * GROUP_W:3 * GROUP_W].astype(F32)
        ua_scr[pl.ds(CONV_A_PAD + r, chunk), :] = a_c * a_h
        va = pc_ref[0, pl.ds(r, chunk), 0:GROUP_W].astype(F32)
        vb = pc_ref[0, pl.ds(r, chunk), GROUP_W:2 * GROUP_W].astype(F32)
        uc_scr[pl.ds(CONV_C_PAD + r, chunk), :] = va * jax.nn.sigmoid(vb)
        return carry

    lax.fori_loop(0, n_chunks, fill, 0)

    def mix(i, carry):
        r = pl.multiple_of(i * chunk, chunk)
        acc = _shifted_conv(ua_scr, wa_ref, r, chunk, CONV_A_W, CONV_A_PAD) + ba_ref[...]
        a_b = pa_ref[0, pl.ds(r, chunk), 0:GROUP_W].astype(F32)
        a_g = pa_ref[0, pl.ds(r, chunk), 3 * GROUP_W:4 * GROUP_W].astype(F32)
        ya_ref[0, pl.ds(r, chunk), :] = (a_b * acc * _silu(a_g)).astype(BF16)
        acc = _shifted_conv(uc_scr, wc_ref, r, chunk, CONV_C_W, CONV_C_PAD) + bc_ref[...]
        mu = jnp.mean(acc, axis=-1, keepdims=True)
        xc = acc - mu
        u = xc * lax.rsqrt(jnp.mean(xc * xc, axis=-1, keepdims=True) + EPS)
        u = _silu(u * lng_ref[...] + lnb_ref[...])
        u = _dot(u.astype(BF16), wpw_ref[...])
        c_g = pc_ref[0, pl.ds(r, chunk), 2 * GROUP_W:3 * GROUP_W].astype(F32)
        yc_ref[0, pl.ds(r, chunk), :] = (u * _silu(c_g)).astype(BF16)
        return carry

    lax.fori_loop(0, n_chunks, mix, 0)


def _conv_mixers(pa, pc, lw, chunk):
    b, s, _ = pa.shape
    seq_blk = lambda width: pl.BlockSpec((1, s, width), lambda i: (i, 0, 0))
    return pl.pallas_call(
        functools.partial(_conv_kernel, seq=s, chunk=chunk),
        out_shape=[jax.ShapeDtypeStruct((b, s, GROUP_W), BF16)] * 2,
        grid=(b,),
        in_specs=[seq_blk(WA), seq_blk(WC),
                  _full((CONV_A_W, GROUP_W)), _full((1, GROUP_W)),
                  _full((CONV_C_W, GROUP_W)), _full((1, GROUP_W)),
                  _full((1, GROUP_W)), _full((1, GROUP_W)), _full((GROUP_W, GROUP_W))],
        out_specs=[seq_blk(GROUP_W)] * 2,
        scratch_shapes=[pltpu.VMEM((s + 2 * CONV_A_PAD, GROUP_W), F32),
                        pltpu.VMEM((s + 2 * CONV_C_PAD, GROUP_W), F32)],
        compiler_params=_params(1),
        name="conv_mixers",
    )(pa, pc, lw["conv_a_w"], lw["conv_a_b"], lw["conf_w_dw"], lw["conf_b_dw"],
      lw["conf_ln_g"], lw["conf_ln_b"], lw["conf_w_pw"])


def _mla_kernel(pb_ref, gkv_ref, wkv_ref, gq_ref, wq_ref, place_ref, ctab_ref, stab_ref,
                yb_ref, q_scr, k_scr, v_scr, *, seq, chunk, tq):
    scale = (MLA_NOPE + MLA_ROPE) ** -0.5
    hw = 2 * HEAD_DIM

    def prep(i, carry):
        r = pl.multiple_of(i * chunk, chunk)
        ctab = ctab_ref[pl.ds(r, chunk), :]
        stab = stab_ref[pl.ds(r, chunk), :]
        kvl = pb_ref[0, pl.ds(r, chunk), GROUP_W:GROUP_W + MLA_KV_RANK].astype(F32)
        kvn = kvl * lax.rsqrt(jnp.mean(kvl * kvl, axis=-1, keepdims=True) + EPS) * gkv_ref[...]
        kv = _dot(kvn.astype(BF16), wkv_ref[...])
        qk = pb_ref[0, pl.ds(r, chunk), GROUP_W + MLA_KV_RANK:WB]
        placed = _dot(qk, place_ref[...])
        k_rot = placed[:, 0:hw] * ctab + placed[:, hw:2 * hw] * stab
        for h in range(N_HEADS):
            k_scr[h, pl.ds(r, chunk), :] = (kv[:, h * hw:(h + 1) * hw] + k_rot).astype(BF16)
        v_scr[pl.ds(r, chunk), :] = kv[:, N_HEADS * hw:].astype(BF16)
        qf = qk.astype(F32)
        lane = lax.broadcasted_iota(jnp.int32, qf.shape, 1)
        ssq = jnp.sum(jnp.where(lane < MLA_Q_RANK, qf * qf, 0.0), axis=-1, keepdims=True)
        qn = qf * lax.rsqrt(ssq * (1.0 / MLA_Q_RANK) + EPS) * gq_ref[...]
        qq = _dot(qn.astype(BF16), wq_ref[...])
        for h in range(N_HEADS):
            q_h = qq[:, h * hw:(h + 1) * hw] * ctab + qq[:, (N_HEADS + h) * hw:(N_HEADS + h + 1) * hw] * stab
            q_scr[h, pl.ds(r, chunk), :] = (q_h * scale).astype(BF16)
        return carry

    lax.fori_loop(0, seq // chunk, prep, 0)

    def attend(i, carry):
        r = pl.multiple_of(i * tq, tq)
        lane_head = lax.broadcasted_iota(jnp.int32, (tq, GROUP_W), 1) // HEAD_DIM
        acc = jnp.zeros((tq, GROUP_W), F32)
        for h in range(N_HEADS):
            s = _dot_nt(q_scr[h, pl.ds(r, tq), :], k_scr[h])
            m = jnp.max(s, axis=-1, keepdims=True)
            p = jnp.exp(s - m)
            denom = jnp.sum(p, axis=-1, keepdims=True)
            o = _dot(p.astype(BF16), v_scr[...])
            acc = jnp.where(lane_head == h, o / denom, acc)
        gate = pb_ref[0, pl.ds(r, tq), 0:GROUP_W].astype(F32)
        yb_ref[0, pl.ds(r, tq), :] = (acc * _silu(gate)).astype(BF16)
        return carry

    lax.fori_loop(0, seq // tq, attend, 0)


def _mla_mixer(pb, lw, tabs, chunk, tq):
    b, s, _ = pb.shape
    hw = 2 * HEAD_DIM
    return pl.pallas_call(
        functools.partial(_mla_kernel, seq=s, chunk=chunk, tq=tq),
        out_shape=jax.ShapeDtypeStruct((b, s, GROUP_W), BF16),
        grid=(b,),
        in_specs=[pl.BlockSpec((1, s, WB), lambda i: (i, 0, 0)),
                  _full((1, MLA_KV_RANK)), _full((MLA_KV_RANK, N_HEADS * hw + GROUP_W)),
                  _full((1, 256)), _full((256, 2 * N_HEADS * hw)), _full((256, 2 * hw)),
                  _full((s, hw)), _full((s, hw))],
        out_specs=pl.BlockSpec((1, s, GROUP_W), lambda i: (i, 0, 0)),
        scratch_shapes=[pltpu.VMEM((N_HEADS, s, hw), BF16), pltpu.VMEM((N_HEADS, s, hw), BF16),
                        pltpu.VMEM((s, GROUP_W), BF16)],
        compiler_params=_params(1),
        name="mla_mixer",
    )(pb, lw["mla_g_kv"], lw["mla_w_kv"], lw["mla_g_q"], lw["mla_w_q"], tabs["place_kr"],
      tabs["ctab"], tabs["stab"])


def _swa_kernel(sink_ref, pd_ref, dup_ref, bias_ref, yd_ref, k_scr, v_scr, *, seq, chunk):
    nb = seq // BLOCK
    band = 3 * BLOCK
    k_scr[0:BLOCK, :] = jnp.zeros((BLOCK, GROUP_W), BF16)
    k_scr[BLOCK + seq:, :] = jnp.zeros((BLOCK, GROUP_W), BF16)
    v_scr[0:BLOCK, :] = jnp.zeros((BLOCK, GROUP_W), BF16)
    v_scr[BLOCK + seq:, :] = jnp.zeros((BLOCK, GROUP_W), BF16)

    def prep(i, carry):
        r = pl.multiple_of(i * chunk, chunk)
        kv = pd_ref[0, pl.ds(r, chunk), GROUP_W:2 * GROUP_W]
        dup = _dot(kv, dup_ref[...])
        k_scr[pl.ds(BLOCK + r, chunk), :] = dup[:, 0:GROUP_W].astype(BF16)
        v_scr[pl.ds(BLOCK + r, chunk), :] = dup[:, GROUP_W:].astype(BF16)
        return carry

    lax.fori_loop(0, seq // chunk, prep, 0)

    def attend(n, carry):
        r = pl.multiple_of(n * BLOCK, BLOCK)
        q = pd_ref[0, pl.ds(r, BLOCK), 0:GROUP_W]
        lane_head = lax.broadcasted_iota(jnp.int32, (BLOCK, GROUP_W), 1) // HEAD_DIM
        q_stack = jnp.concatenate(
            [jnp.where(lane_head == h, q, jnp.zeros_like(q)) for h in range(N_HEADS)], axis=0)
        sc = _dot_nt(q_stack, k_scr[pl.ds(r, band), :]) * (HEAD_DIM ** -0.5)
        qi = lax.broadcasted_iota(jnp.int32, (BLOCK, band), 0)
        kj = lax.broadcasted_iota(jnp.int32, (BLOCK, band), 1)
        key_pos = kj + (r - BLOCK)
        valid = (kj >= qi) & (kj <= qi + 2 * WINDOW) & (key_pos >= 0) & (key_pos < seq)
        e_parts, den_parts = [], []
        for h in range(N_HEADS):
            sink = sink_ref[h]
            s_h = jnp.where(valid, sc[h * BLOCK:(h + 1) * BLOCK] + bias_ref[h], NEG)
            m = jnp.maximum(jnp.max(s_h, axis=-1, keepdims=True), sink)
            e = jnp.exp(s_h - m)
            den_parts.append(jnp.sum(e, axis=-1, keepdims=True) + jnp.exp(sink - m))
            e_parts.append(e.astype(BF16))
        o = _dot(jnp.concatenate(e_parts, axis=0), v_scr[pl.ds(r, band), :])
        acc = jnp.zeros((BLOCK, GROUP_W), F32)
        for h in range(N_HEADS):
            acc = jnp.where(lane_head == h, o[h * BLOCK:(h + 1) * BLOCK] / den_parts[h], acc)
        gate = pd_ref[0, pl.ds(r, BLOCK), 2 * GROUP_W:3 * GROUP_W].astype(F32)
        yd_ref[0, pl.ds(r, BLOCK), :] = (acc * _silu(gate)).astype(BF16)
        return carry

    lax.fori_loop(0, nb, attend, 0)


def _swa_mixer(pd, sink, tabs, chunk):
    b, s, _ = pd.shape
    return pl.pallas_call(
        functools.partial(_swa_kernel, seq=s, chunk=chunk),
        out_shape=jax.ShapeDtypeStruct((b, s, GROUP_W), BF16),
        grid_spec=pltpu.PrefetchScalarGridSpec(
            num_scalar_prefetch=1,
            grid=(b,),
            in_specs=[pl.BlockSpec((1, s, WD), lambda i, sink: (i, 0, 0)),
                      pl.BlockSpec((GROUP_W, 2 * GROUP_W), lambda i, sink: (0, 0)),
                      pl.BlockSpec((N_HEADS, BLOCK, 3 * BLOCK), lambda i, sink: (0, 0, 0))],
            out_specs=pl.BlockSpec((1, s, GROUP_W), lambda i, sink: (i, 0, 0)),
            scratch_shapes=[pltpu.VMEM((s + 2 * BLOCK, GROUP_W), BF16),
                            pltpu.VMEM((s + 2 * BLOCK, GROUP_W), BF16)]),
        compiler_params=_params(1),
        name="swa_mixer",
    )(sink, pd, tabs["dup_kv"], tabs["rel_bias"])


def _out_proj_kernel(x_ref, ya_ref, yb_ref, yc_ref, yd_ref, w_ref, g_ref, o_ref):
    out = _dot(ya_ref[...], w_ref[0:GROUP_W, :])
    for j, ref in enumerate((yb_ref, yc_ref, yd_ref), start=1):
        out = out + _dot(ref[...], w_ref[j * GROUP_W:(j + 1) * GROUP_W, :])
    n = out * lax.rsqrt(jnp.mean(out * out, axis=-1, keepdims=True) + EPS) * g_ref[...]
    o_ref[...] = x_ref[...] + n


def _out_proj(x2d, ys, w_out, g_post, tm):
    t = x2d.shape[0]
    row = lambda width: pl.BlockSpec((tm, width), lambda i: (i, 0))
    return pl.pallas_call(
        _out_proj_kernel,
        out_shape=jax.ShapeDtypeStruct((t, D_MODEL), F32),
        grid=(t // tm,),
        in_specs=[row(D_MODEL)] + [row(GROUP_W)] * 4 + [_full((D_MODEL, D_MODEL)), _full((1, D_MODEL))],
        out_specs=row(D_MODEL),
        compiler_params=_params(1),
        name="out_proj",
    )(x2d, *ys, w_out, g_post)


def _t5_buckets(rel):
    half = N_BUCKETS // 2
    max_exact = half // 2
    n = np.abs(rel)
    large = max_exact + (np.log(np.maximum(n, 1) / max_exact) / math.log(MAX_DIST / max_exact)
                         * (half - max_exact)).astype(np.int32)
    large = np.minimum(large, half - 1)
    return (rel > 0).astype(np.int32) * half + np.where(n < max_exact, n, large).astype(np.int32)


def _swap_halves(t):
    return jnp.concatenate([t[..., MLA_ROPE // 2:], t[..., :MLA_ROPE // 2]], axis=-1)


def _layer_weights(l, g_pre, g_post, w_in, w_out, conv_a_w, conv_a_b, mla_g_q, mla_w_q_up, mla_g_kv,
                   mla_w_kv_up, conf_w_dw, conf_b_dw, conf_ln_g, conf_ln_b, conf_w_pw, swa_sink):
    sizes = (256, 256, 256, 256, MLA_Q_RANK, MLA_KV_RANK, MLA_ROPE, 256, 256, 256, 256, 256, 128, 128, 256)
    (a_b, a_c, a_h, a_g, b_ql, b_kvl, b_kr, b_g, c_va, c_vb, c_g, d_q, d_k, d_v, d_g) = jnp.split(
        w_in[l], [int(v) for v in np.cumsum(sizes)[:-1]], axis=-1)
    w_in_r = jnp.concatenate(
        [a_b, a_c, a_h, a_g, b_g, b_kvl, b_ql, b_kr, _swap_halves(b_kr), c_va, c_vb, c_g, d_q, d_k, d_v, d_g],
        axis=-1).astype(BF16)
    hw = 2 * HEAD_DIM
    wq = mla_w_q_up[l].reshape(MLA_Q_RANK, N_HEADS, MLA_NOPE + MLA_ROPE)
    zq = jnp.zeros((MLA_Q_RANK, N_HEADS, hw - MLA_NOPE - MLA_ROPE), F32)
    wq_plain = jnp.concatenate([wq, zq], axis=-1)
    wq_swap = jnp.concatenate(
        [jnp.zeros((MLA_Q_RANK, N_HEADS, MLA_NOPE), F32), _swap_halves(wq[..., MLA_NOPE:]), zq], axis=-1)
    wq_all = jnp.concatenate([wq_plain.reshape(MLA_Q_RANK, -1), wq_swap.reshape(MLA_Q_RANK, -1)], axis=-1)
    wq_all = jnp.pad(wq_all, ((0, 256 - MLA_Q_RANK), (0, 0))).astype(BF16)
    wkv = mla_w_kv_up[l].reshape(MLA_KV_RANK, N_HEADS, MLA_NOPE + HEAD_DIM)
    wk = jnp.concatenate([wkv[..., :MLA_NOPE], jnp.zeros((MLA_KV_RANK, N_HEADS, hw - MLA_NOPE), F32)], axis=-1)
    wkv_all = jnp.concatenate([wk.reshape(MLA_KV_RANK, -1), wkv[..., MLA_NOPE:].reshape(MLA_KV_RANK, -1)],
                              axis=-1).astype(BF16)
    row = lambda v: v.reshape(1, -1).astype(F32)
    return {
        "g_pre": row(g_pre[l]), "g_post": row(g_post[l]), "w_in": w_in_r, "w_out": w_out[l].astype(BF16),
        "conv_a_w": conv_a_w[l], "conv_a_b": row(conv_a_b[l]),
        "conf_w_dw": conf_w_dw[l], "conf_b_dw": row(conf_b_dw[l]),
        "conf_ln_g": row(conf_ln_g[l]), "conf_ln_b": row(conf_ln_b[l]), "conf_w_pw": conf_w_pw[l].astype(BF16),
        "mla_g_kv": row(mla_g_kv[l]), "mla_w_kv": wkv_all,
        "mla_g_q": row(jnp.pad(mla_g_q[l], (0, 256 - MLA_Q_RANK))), "mla_w_q": wq_all,
        "swa_sink": swa_sink[l].astype(F32),
    }


def _tables(seq, rel_bias):
    hw = 2 * HEAD_DIM
    inv = ROPE_THETA ** (-jnp.arange(0, MLA_ROPE, 2, dtype=F32) / MLA_ROPE)
    ang = jnp.arange(seq, dtype=F32)[:, None] * inv[None, :]
    cos, sin = jnp.cos(ang), jnp.sin(ang)
    ctab = jnp.concatenate([jnp.ones((seq, MLA_NOPE), F32), cos, cos,
                            jnp.zeros((seq, hw - MLA_NOPE - MLA_ROPE), F32)], axis=-1)
    stab = jnp.concatenate([jnp.zeros((seq, MLA_NOPE), F32), -sin, sin,
                            jnp.zeros((seq, hw - MLA_NOPE - MLA_ROPE), F32)], axis=-1)
    place = np.zeros((256, 2 * hw), np.float32)
    for j in range(MLA_ROPE):
        place[MLA_Q_RANK + j, MLA_NOPE + j] = 1.0
        place[MLA_Q_RANK + MLA_ROPE + j, hw + MLA_NOPE + j] = 1.0
    dup = np.zeros((GROUP_W, 2 * GROUP_W), np.float32)
    for h in range(N_HEADS):
        for j in range(HEAD_DIM):
            dup[(h // 2) * HEAD_DIM + j, h * HEAD_DIM + j] = 1.0
            dup[2 * HEAD_DIM + (h // 2) * HEAD_DIM + j, GROUP_W + h * HEAD_DIM + j] = 1.0
    rel = np.arange(3 * BLOCK)[None, :] - BLOCK - np.arange(BLOCK)[:, None]
    bias = rel_bias.astype(F32)[_t5_buckets(rel)].transpose(2, 0, 1)
    return {"ctab": ctab, "stab": stab, "place_kr": jnp.asarray(place, BF16),
            "dup_kv": jnp.asarray(dup, BF16), "rel_bias": bias}


def _trunk(x, layers, tabs):
    b, s, d = x.shape
    tm = min(512, b * s)
    chunk = min(256, s)
    x2d = x.reshape(b * s, d)
    for lw in layers:
        pa, pb, pc, pd = _in_proj(x2d, lw["g_pre"], lw["w_in"], tm)
        pa, pb, pc, pd = (p.reshape(b, s, -1) for p in (pa, pb, pc, pd))
        ya, yc = _conv_mixers(pa, pc, lw, min(128, s))
        yb = _mla_mixer(pb, lw, tabs, chunk, min(128, s))
        yd = _swa_mixer(pd, lw["swa_sink"], tabs, chunk)
        ys = [y.reshape(b * s, GROUP_W) for y in (ya, yb, yc, yd)]
        x2d = _out_proj(x2d, ys, lw["w_out"], lw["g_post"], tm)
    return x2d.reshape(b, s, d)


def kernel(x_prompt, x_sample, g_pre, g_post, w_in, w_out, conv_a_w, conv_a_b, mla_g_q, mla_w_q_up, mla_g_kv,
           mla_w_kv_up, conf_w_dw, conf_b_dw, conf_ln_g, conf_ln_b, conf_w_pw, swa_sink, rel_bias):
    depth = w_in.shape[0]
    layers = [_layer_weights(l, g_pre, g_post, w_in, w_out, conv_a_w, conv_a_b, mla_g_q, mla_w_q_up, mla_g_kv,
                             mla_w_kv_up, conf_w_dw, conf_b_dw, conf_ln_g, conf_ln_b, conf_w_pw, swa_sink)
              for l in range(depth)]
    outs = []
    for x in (x_prompt, x_sample):
        tabs = _tables(x.shape[1], rel_bias)
        outs.append(_trunk(x, layers, tabs))
    return tuple(outs)
```

```python
import functools
import math

import jax
import jax.numpy as jnp
import numpy as np
from jax import lax
from jax.experimental import pallas as pl
from jax.experimental.pallas import tpu as pltpu

F32 = jnp.float32
BF16 = jnp.bfloat16

D_MODEL = 1024
GROUP_W = 256
HEAD_DIM = 64
N_HEADS = 4
CONV_A_W = 3
CONV_C_W = 31
MLA_NOPE = 64
MLA_ROPE = 32
MLA_Q_RANK = 192
MLA_KV_RANK = 128
ROPE_THETA = 10000.0
SWA_KV_HEADS = 2
WINDOW = 128
BLOCK = 128
N_BUCKETS = 32
MAX_DIST = 128
EPS = 1e-6
NEG = -1e30
LOG2_E = 1.4426950408889634

WA = 4 * GROUP_W
WB = GROUP_W + MLA_KV_RANK + 256
WC = 3 * GROUP_W
WD = 3 * GROUP_W
W_IN_PAD = WA + WB + WC + WD

VMEM_LIMIT_BYTES = 56 * 1024 * 1024
CONV_A_PAD = 8
CONV_C_PAD = 16


def _silu(x):
    return x * jax.nn.sigmoid(x)


def _dot(a, b):
    return jnp.dot(a, b, preferred_element_type=F32)


def _dot_nt(a, b):
    return lax.dot_general(a, b, (((1,), (1,)), ((), ())), preferred_element_type=F32)


def _params(n_grid_dims):
    return pltpu.CompilerParams(
        dimension_semantics=("parallel",) * n_grid_dims,
        vmem_limit_bytes=VMEM_LIMIT_BYTES)


def _full(shape):
    return pl.BlockSpec(shape, lambda *_: (0,) * len(shape))


def _in_proj_kernel(x_ref, g_ref, w_ref, pa_ref, pb_ref, pc_ref, pd_ref):
    x = x_ref[...]
    h = x * lax.rsqrt(jnp.mean(x * x, axis=-1, keepdims=True) + EPS) * g_ref[...]
    h = h.astype(BF16)
    off = 0
    for ref, width in ((pa_ref, WA), (pb_ref, WB), (pc_ref, WC), (pd_ref, WD)):
        ref[...] = _dot(h, w_ref[:, off:off + width]).astype(BF16)
        off += width


def _in_proj(x2d, g_pre, w_in, tm):
    t = x2d.shape[0]
    row = lambda width: pl.BlockSpec((tm, width), lambda i: (i, 0))
    return pl.pallas_call(
        _in_proj_kernel,
        out_shape=[jax.ShapeDtypeStruct((t, w), BF16) for w in (WA, WB, WC, WD)],
        grid=(t // tm,),
        in_specs=[row(D_MODEL), _full((1, D_MODEL)), _full((D_MODEL, W_IN_PAD))],
        out_specs=[row(WA), row(WB), row(WC), row(WD)],
        compiler_params=_params(1),
        name="in_proj",
    )(x2d, g_pre, w_in)


SUBLANES = 8


def _shifted_conv(scr, w_ref, r, chunk, width, pad):
    c0 = pad - (width - 1) // 2
    rows = chunk + SUBLANES
    n_a = (c0 + width - 1) // SUBLANES + 1
    tiles = [scr[pl.ds(r + SUBLANES * a, rows), :] for a in range(n_a)]
    acc = None
    for b in range(SUBLANES):
        z = None
        for k in range(width):
            if (k + c0) % SUBLANES != b:
                continue
            term = tiles[(k + c0) // SUBLANES] * w_ref[k:k + 1, :]
            z = term if z is None else z + term
        if z is None:
            continue
        z = z[b:b + chunk]
        acc = z if acc is None else acc + z
    return acc


def _conv_kernel(pa_ref, pc_ref, wa_ref, ba_ref, wc_ref, bc_ref, lng_ref, lnb_ref, wpw_ref,
                 ya_ref, yc_ref, ua_scr, uc_scr, *, seq, chunk):
    n_chunks = seq // chunk
    ua_scr[0:CONV_A_PAD, :] = jnp.zeros((CONV_A_PAD, GROUP_W), F32)
    ua_scr[CONV_A_PAD + seq:, :] = jnp.zeros((CONV_A_PAD, GROUP_W), F32)
    uc_scr[0:CONV_C_PAD, :] = jnp.zeros((CONV_C_PAD, GROUP_W), F32)
    uc_scr[CONV_C_PAD + seq:, :] = jnp.zeros((CONV_C_PAD, GROUP_W), F32)

    def fill(i, carry):
        r = pl.multiple_of(i * chunk, chunk)
        a_c = pa_ref[0, pl.ds(r, chunk), GROUP_W:2 * GROUP_W].astype(F32)
        a_h = pa_ref[0, pl.ds(r, chunk), 2 * GROUP_W:3 * GROUP_W].astype(F32)
        ua_scr[pl.ds(CONV_A_PAD + r, chunk), :] = a_c * a_h
        va = pc_ref[0, pl.ds(r, chunk), 0:GROUP_W].astype(F32)
        vb = pc_ref[0, pl.ds(r, chunk), GROUP_W:2 * GROUP_W].astype(F32)
        uc_scr[pl.ds(CONV_C_PAD + r, chunk), :] = va * jax.nn.sigmoid(vb)
        return carry

    lax.fori_loop(0, n_chunks, fill, 0)

    def mix(i, carry):
        r = pl.multiple_of(i * chunk, chunk)
        acc = _shifted_conv(ua_scr, wa_ref, r, chunk, CONV_A_W, CONV_A_PAD) + ba_ref[...]
        a_b = pa_ref[0, pl.ds(r, chunk), 0:GROUP_W].astype(F32)
        a_g = pa_ref[0, pl.ds(r, chunk), 3 * GROUP_W:4 * GROUP_W].astype(F32)
        ya_ref[0, pl.ds(r, chunk), :] = (a_b * acc * _silu(a_g)).astype(BF16)
        acc = _shifted_conv(uc_scr, wc_ref, r, chunk, CONV_C_W, CONV_C_PAD) + bc_ref[...]
        mu = jnp.mean(acc, axis=-1, keepdims=True)
        xc = acc - mu
        u = xc * lax.rsqrt(jnp.mean(xc * xc, axis=-1, keepdims=True) + EPS)
        u = _silu(u * lng_ref[...] + lnb_ref[...])
        u = _dot(u.astype(BF16), wpw_ref[...])
        c_g = pc_ref[0, pl.ds(r, chunk), 2 * GROUP_W:3 * GROUP_W].astype(F32)
        yc_ref[0, pl.ds(r, chunk), :] = (u * _silu(c_g)).astype(BF16)
        return carry

    lax.fori_loop(0, n_chunks, mix, 0, unroll=2)


def _conv_mixers(pa, pc, lw, chunk):
    b, s, _ = pa.shape
    seq_blk = lambda width: pl.BlockSpec((1, s, width), lambda i: (i, 0, 0))
    return pl.pallas_call(
        functools.partial(_conv_kernel, seq=s, chunk=chunk),
        out_shape=[jax.ShapeDtypeStruct((b, s, GROUP_W), BF16)] * 2,
        grid=(b,),
        in_specs=[seq_blk(WA), seq_blk(WC),
                  _full((CONV_A_W, GROUP_W)), _full((1, GROUP_W)),
                  _full((CONV_C_W, GROUP_W)), _full((1, GROUP_W)),
                  _full((1, GROUP_W)), _full((1, GROUP_W)), _full((GROUP_W, GROUP_W))],
        out_specs=[seq_blk(GROUP_W)] * 2,
        scratch_shapes=[pltpu.VMEM((s + 2 * CONV_A_PAD, GROUP_W), F32),
                        pltpu.VMEM((s + 2 * CONV_C_PAD, GROUP_W), F32)],
        compiler_params=_params(1),
        name="conv_mixers",
    )(pa, pc, lw["conv_a_w"], lw["conv_a_b"], lw["conf_w_dw"], lw["conf_b_dw"],
      lw["conf_ln_g"], lw["conf_ln_b"], lw["conf_w_pw"])


def _mla_kernel(pb_ref, gkv_ref, wkv_ref, gq_ref, wq_ref, place_ref, ctab_ref, stab_ref,
                yb_ref, q_scr, kt_scr, v_scr, s_scr, *, seq, chunk, tq):
    assert N_HEADS % 2 == 0
    scale = (MLA_NOPE + MLA_ROPE) ** -0.5 * LOG2_E
    hw = 2 * HEAD_DIM

    def prep(i, carry):
        r = pl.multiple_of(i * chunk, chunk)
        ctab = ctab_ref[pl.ds(r, chunk), :]
        stab = stab_ref[pl.ds(r, chunk), :]
        kvl = pb_ref[0, pl.ds(r, chunk), GROUP_W:GROUP_W + MLA_KV_RANK].astype(F32)
        kvn = kvl * lax.rsqrt(jnp.mean(kvl * kvl, axis=-1, keepdims=True) + EPS) * gkv_ref[...]
        kv = _dot(kvn.astype(BF16), wkv_ref[...])
        qk = pb_ref[0, pl.ds(r, chunk), GROUP_W + MLA_KV_RANK:WB]
        placed = _dot(qk, place_ref[...])
        k_rot = placed[:, 0:hw] * ctab + placed[:, hw:2 * hw] * stab
        for h in range(N_HEADS):
            k_h = kv[:, h * hw:(h + 1) * hw] + k_rot
            kt_scr[h, :, pl.ds(r, chunk)] = k_h.T.astype(BF16)
        v_scr[pl.ds(r, chunk), :] = kv[:, N_HEADS * hw:].astype(BF16)
        qf = qk.astype(F32)
        lane = lax.broadcasted_iota(jnp.int32, qf.shape, 1)
        ssq = jnp.sum(jnp.where(lane < MLA_Q_RANK, qf * qf, 0.0), axis=-1, keepdims=True)
        qn = qf * lax.rsqrt(ssq * (1.0 / MLA_Q_RANK) + EPS) * gq_ref[...]
        qq = _dot(qn.astype(BF16), wq_ref[...])
        for h in range(N_HEADS):
            q_h = qq[:, h * hw:(h + 1) * hw] * ctab + qq[:, (N_HEADS + h) * hw:(N_HEADS + h + 1) * hw] * stab
            q_scr[h, pl.ds(r, chunk), :] = (q_h * scale).astype(BF16)
        return carry

    lax.fori_loop(0, seq // chunk, prep, 0)

    def scores(h, row, slot):
        s_scr[slot] = _dot(q_scr[h, pl.ds(row, tq), :], kt_scr[h])

    scores(0, 0, 0)

    def attend(i, carry):
        r = pl.multiple_of(i * tq, tq)
        r_next = pl.multiple_of(jnp.minimum(r + tq, seq - tq), tq)
        lane_head = lax.broadcasted_iota(jnp.int32, (tq, GROUP_W), 1) // HEAD_DIM
        acc = jnp.zeros((tq, GROUP_W), F32)
        for h in range(N_HEADS):
            if h + 1 < N_HEADS:
                scores(h + 1, r, (h + 1) % 2)
            else:
                scores(0, r_next, 0)
            s = s_scr[h % 2]
            m = jnp.max(s, axis=-1, keepdims=True)
            p = jnp.exp2(s - m)
            denom = jnp.sum(p, axis=-1, keepdims=True)
            o = _dot(p.astype(BF16), v_scr[...])
            acc = jnp.where(lane_head == h, o / denom, acc)
        gate = pb_ref[0, pl.ds(r, tq), 0:GROUP_W].astype(F32)
        yb_ref[0, pl.ds(r, tq), :] = (acc * _silu(gate)).astype(BF16)
        return carry

    lax.fori_loop(0, seq // tq, attend, 0)


def _mla_mixer(pb, lw, tabs, chunk, tq):
    b, s, _ = pb.shape
    hw = 2 * HEAD_DIM
    return pl.pallas_call(
        functools.partial(_mla_kernel, seq=s, chunk=chunk, tq=tq),
        out_shape=jax.ShapeDtypeStruct((b, s, GROUP_W), BF16),
        grid=(b,),
        in_specs=[pl.BlockSpec((1, s, WB), lambda i: (i, 0, 0)),
                  _full((1, MLA_KV_RANK)), _full((MLA_KV_RANK, N_HEADS * hw + GROUP_W)),
                  _full((1, 256)), _full((256, 2 * N_HEADS * hw)), _full((256, 2 * hw)),
                  _full((s, hw)), _full((s, hw))],
        out_specs=pl.BlockSpec((1, s, GROUP_W), lambda i: (i, 0, 0)),
        scratch_shapes=[pltpu.VMEM((N_HEADS, s, hw), BF16), pltpu.VMEM((N_HEADS, hw, s), BF16),
                        pltpu.VMEM((s, GROUP_W), BF16), pltpu.VMEM((2, tq, s), F32)],
        compiler_params=_params(1),
        name="mla_mixer",
    )(pb, lw["mla_g_kv"], lw["mla_w_kv"], lw["mla_g_q"], lw["mla_w_q"], tabs["place_kr"],
      tabs["ctab"], tabs["stab"])


def _swa_kernel(sink_ref, pd_ref, dup_ref, bias_ref, yd_ref, k_scr, v_scr, *, seq, chunk):
    nb = seq // BLOCK
    band = 3 * BLOCK
    k_scr[0:BLOCK, :] = jnp.zeros((BLOCK, GROUP_W), BF16)
    k_scr[BLOCK + seq:, :] = jnp.zeros((BLOCK, GROUP_W), BF16)
    v_scr[0:BLOCK, :] = jnp.zeros((BLOCK, GROUP_W), BF16)
    v_scr[BLOCK + seq:, :] = jnp.zeros((BLOCK, GROUP_W), BF16)

    def prep(i, carry):
        r = pl.multiple_of(i * chunk, chunk)
        kv = pd_ref[0, pl.ds(r, chunk), GROUP_W:2 * GROUP_W]
        dup = _dot(kv, dup_ref[...])
        k_scr[pl.ds(BLOCK + r, chunk), :] = dup[:, 0:GROUP_W].astype(BF16)
        v_scr[pl.ds(BLOCK + r, chunk), :] = dup[:, GROUP_W:].astype(BF16)
        return carry

    lax.fori_loop(0, seq // chunk, prep, 0)

    def attend(n, carry):
        r = pl.multiple_of(n * BLOCK, BLOCK)
        q = pd_ref[0, pl.ds(r, BLOCK), 0:GROUP_W] * (HEAD_DIM ** -0.5)
        lane_head = lax.broadcasted_iota(jnp.int32, (BLOCK, GROUP_W), 1) // HEAD_DIM
        q_stack = jnp.concatenate(
            [jnp.where(lane_head == h, q, jnp.zeros_like(q)) for h in range(N_HEADS)], axis=0)
        sc = _dot_nt(q_stack, k_scr[pl.ds(r, band), :])
        variant = jnp.where(n == 0, 0, jnp.where(n == nb - 1, 2, 1))
        e_parts, den_parts = [], []
        for h in range(N_HEADS):
            sink = sink_ref[h]
            s_h = sc[h * BLOCK:(h + 1) * BLOCK] + bias_ref[variant, h]
            m = jnp.maximum(jnp.max(s_h, axis=-1, keepdims=True), sink)
            e = jnp.exp(s_h - m)
            den_parts.append(jnp.sum(e, axis=-1, keepdims=True) + jnp.exp(sink - m))
            e_parts.append(e.astype(BF16))
        o = _dot(jnp.concatenate(e_parts, axis=0), v_scr[pl.ds(r, band), :])
        acc = jnp.zeros((BLOCK, GROUP_W), F32)
        for h in range(N_HEADS):
            acc = jnp.where(lane_head == h, o[h * BLOCK:(h + 1) * BLOCK] / den_parts[h], acc)
        gate = pd_ref[0, pl.ds(r, BLOCK), 2 * GROUP_W:3 * GROUP_W].astype(F32)
        yd_ref[0, pl.ds(r, BLOCK), :] = (acc * _silu(gate)).astype(BF16)
        return carry

    lax.fori_loop(0, nb, attend, 0, unroll=2)


def _swa_mixer(pd, sink, tabs, chunk):
    b, s, _ = pd.shape
    return pl.pallas_call(
        functools.partial(_swa_kernel, seq=s, chunk=chunk),
        out_shape=jax.ShapeDtypeStruct((b, s, GROUP_W), BF16),
        grid_spec=pltpu.PrefetchScalarGridSpec(
            num_scalar_prefetch=1,
            grid=(b,),
            in_specs=[pl.BlockSpec((1, s, WD), lambda i, sink: (i, 0, 0)),
                      pl.BlockSpec((GROUP_W, 2 * GROUP_W), lambda i, sink: (0, 0)),
                      pl.BlockSpec((3, N_HEADS, BLOCK, 3 * BLOCK), lambda i, sink: (0, 0, 0, 0))],
            out_specs=pl.BlockSpec((1, s, GROUP_W), lambda i, sink: (i, 0, 0)),
            scratch_shapes=[pltpu.VMEM((s + 2 * BLOCK, GROUP_W), BF16),
                            pltpu.VMEM((s + 2 * BLOCK, GROUP_W), BF16)]),
        compiler_params=_params(1),
        name="swa_mixer",
    )(sink, pd, tabs["dup_kv"], tabs["rel_bias"])


def _out_proj_kernel(x_ref, ya_ref, yb_ref, yc_ref, yd_ref, w_ref, g_ref, o_ref):
    out = _dot(ya_ref[...], w_ref[0:GROUP_W, :])
    for j, ref in enumerate((yb_ref, yc_ref, yd_ref), start=1):
        out = out + _dot(ref[...], w_ref[j * GROUP_W:(j + 1) * GROUP_W, :])
    n = out * lax.rsqrt(jnp.mean(out * out, axis=-1, keepdims=True) + EPS) * g_ref[...]
    o_ref[...] = x_ref[...] + n


def _out_proj(x2d, ys, w_out, g_post, tm):
    t = x2d.shape[0]
    row = lambda width: pl.BlockSpec((tm, width), lambda i: (i, 0))
    return pl.pallas_call(
        _out_proj_kernel,
        out_shape=jax.ShapeDtypeStruct((t, D_MODEL), F32),
        grid=(t // tm,),
        in_specs=[row(D_MODEL)] + [row(GROUP_W)] * 4 + [_full((D_MODEL, D_MODEL)), _full((1, D_MODEL))],
        out_specs=row(D_MODEL),
        compiler_params=_params(1),
        name="out_proj",
    )(x2d, *ys, w_out, g_post)


def _t5_buckets(rel):
    half = N_BUCKETS // 2
    max_exact = half // 2
    n = np.abs(rel)
    large = max_exact + (np.log(np.maximum(n, 1) / max_exact) / math.log(MAX_DIST / max_exact)
                         * (half - max_exact)).astype(np.int32)
    large = np.minimum(large, half - 1)
    return (rel > 0).astype(np.int32) * half + np.where(n < max_exact, n, large).astype(np.int32)


def _swap_halves(t):
    return jnp.concatenate([t[..., MLA_ROPE // 2:], t[..., :MLA_ROPE // 2]], axis=-1)


def _layer_weights(l, g_pre, g_post, w_in, w_out, conv_a_w, conv_a_b, mla_g_q, mla_w_q_up, mla_g_kv,
                   mla_w_kv_up, conf_w_dw, conf_b_dw, conf_ln_g, conf_ln_b, conf_w_pw, swa_sink):
    sizes = (256, 256, 256, 256, MLA_Q_RANK, MLA_KV_RANK, MLA_ROPE, 256, 256, 256, 256, 256, 128, 128, 256)
    (a_b, a_c, a_h, a_g, b_ql, b_kvl, b_kr, b_g, c_va, c_vb, c_g, d_q, d_k, d_v, d_g) = jnp.split(
        w_in[l], [int(v) for v in np.cumsum(sizes)[:-1]], axis=-1)
    w_in_r = jnp.concatenate(
        [a_b, a_c, a_h, a_g, b_g, b_kvl, b_ql, b_kr, _swap_halves(b_kr), c_va, c_vb, c_g, d_q, d_k, d_v, d_g],
        axis=-1).astype(BF16)
    hw = 2 * HEAD_DIM
    wq = mla_w_q_up[l].reshape(MLA_Q_RANK, N_HEADS, MLA_NOPE + MLA_ROPE)
    zq = jnp.zeros((MLA_Q_RANK, N_HEADS, hw - MLA_NOPE - MLA_ROPE), F32)
    wq_plain = jnp.concatenate([wq, zq], axis=-1)
    wq_swap = jnp.concatenate(
        [jnp.zeros((MLA_Q_RANK, N_HEADS, MLA_NOPE), F32), _swap_halves(wq[..., MLA_NOPE:]), zq], axis=-1)
    wq_all = jnp.concatenate([wq_plain.reshape(MLA_Q_RANK, -1), wq_swap.reshape(MLA_Q_RANK, -1)], axis=-1)
    wq_all = jnp.pad(wq_all, ((0, 256 - MLA_Q_RANK), (0, 0))).astype(BF16)
    wkv = mla_w_kv_up[l].reshape(MLA_KV_RANK, N_HEADS, MLA_NOPE + HEAD_DIM)
    wk = jnp.concatenate([wkv[..., :MLA_NOPE], jnp.zeros((MLA_KV_RANK, N_HEADS, hw - MLA_NOPE), F32)], axis=-1)
    wkv_all = jnp.concatenate([wk.reshape(MLA_KV_RANK, -1), wkv[..., MLA_NOPE:].reshape(MLA_KV_RANK, -1)],
                              axis=-1).astype(BF16)
    row = lambda v: v.reshape(1, -1).astype(F32)
    return {
        "g_pre": row(g_pre[l]), "g_post": row(g_post[l]), "w_in": w_in_r, "w_out": w_out[l].astype(BF16),
        "conv_a_w": conv_a_w[l], "conv_a_b": row(conv_a_b[l]),
        "conf_w_dw": conf_w_dw[l], "conf_b_dw": row(conf_b_dw[l]),
        "conf_ln_g": row(conf_ln_g[l]), "conf_ln_b": row(conf_ln_b[l]), "conf_w_pw": conf_w_pw[l].astype(BF16),
        "mla_g_kv": row(mla_g_kv[l]), "mla_w_kv": wkv_all,
        "mla_g_q": row(jnp.pad(mla_g_q[l], (0, 256 - MLA_Q_RANK))), "mla_w_q": wq_all,
        "swa_sink": swa_sink[l].astype(F32),
    }


def _tables(seq, rel_bias):
    hw = 2 * HEAD_DIM
    inv = ROPE_THETA ** (-jnp.arange(0, MLA_ROPE, 2, dtype=F32) / MLA_ROPE)
    ang = jnp.arange(seq, dtype=F32)[:, None] * inv[None, :]
    cos, sin = jnp.cos(ang), jnp.sin(ang)
    ctab = jnp.concatenate([jnp.ones((seq, MLA_NOPE), F32), cos, cos,
                            jnp.zeros((seq, hw - MLA_NOPE - MLA_ROPE), F32)], axis=-1)
    stab = jnp.concatenate([jnp.zeros((seq, MLA_NOPE), F32), -sin, sin,
                            jnp.zeros((seq, hw - MLA_NOPE - MLA_ROPE), F32)], axis=-1)
    place = np.zeros((256, 2 * hw), np.float32)
    for j in range(MLA_ROPE):
        place[MLA_Q_RANK + j, MLA_NOPE + j] = 1.0
        place[MLA_Q_RANK + MLA_ROPE + j, hw + MLA_NOPE + j] = 1.0
    dup = np.zeros((GROUP_W, 2 * GROUP_W), np.float32)
    for h in range(N_HEADS):
        for j in range(HEAD_DIM):
            dup[(h // 2) * HEAD_DIM + j, h * HEAD_DIM + j] = 1.0
            dup[2 * HEAD_DIM + (h // 2) * HEAD_DIM + j, GROUP_W + h * HEAD_DIM + j] = 1.0
    assert seq >= 2 * BLOCK
    key_col = np.arange(3 * BLOCK)[None, :]
    rel = key_col - BLOCK - np.arange(BLOCK)[:, None]
    buckets = _t5_buckets(rel)
    bias = jnp.zeros((N_HEADS, BLOCK, 3 * BLOCK), F32)
    for bkt in range(N_BUCKETS):
        bias = jnp.where(jnp.asarray(buckets == bkt)[None], rel_bias[bkt].astype(F32)[:, None, None], bias)
    in_window = np.abs(rel) <= WINDOW
    valid = np.stack([in_window & (key_col >= BLOCK), in_window, in_window & (key_col < 2 * BLOCK)])
    bias = jnp.where(jnp.asarray(valid)[:, None], bias[None], NEG)
    return {"ctab": ctab, "stab": stab, "place_kr": jnp.asarray(place, BF16),
            "dup_kv": jnp.asarray(dup, BF16), "rel_bias": bias}


def _trunk(x, layers, tabs):
    b, s, d = x.shape
    tm = min(512, b * s)
    chunk = min(256, s)
    x2d = x.reshape(b * s, d)
    for lw in layers:
        pa, pb, pc, pd = _in_proj(x2d, lw["g_pre"], lw["w_in"], tm)
        pa, pb, pc, pd = (p.reshape(b, s, -1) for p in (pa, pb, pc, pd))
        ya, yc = _conv_mixers(pa, pc, lw, min(128, s))
        yb = _mla_mixer(pb, lw, tabs, chunk, min(256, s))
        yd = _swa_mixer(pd, lw["swa_sink"], tabs, chunk)
        ys = [y.reshape(b * s, GROUP_W) for y in (ya, yb, yc, yd)]
        x2d = _out_proj(x2d, ys, lw["w_out"], lw["g_post"], tm)
    return x2d.reshape(b, s, d)


def kernel(x_prompt, x_sample, g_pre, g_post, w_in, w_out, conv_a_w, conv_a_b, mla_g_q, mla_w_q_up, mla_g_kv,
           mla_w_kv_up, conf_w_dw, conf_b_dw, conf_ln_g, conf_ln_b, conf_w_pw, swa_sink, rel_bias):
    depth = w_in.shape[0]
    layers = [_layer_weights(l, g_pre, g_post, w_in, w_out, conv_a_w, conv_a_b, mla_g_q, mla_w_q_up, mla_g_kv,
                             mla_w_kv_up, conf_w_dw, conf_b_dw, conf_ln_g, conf_ln_b, conf_w_pw, swa_sink)
              for l in range(depth)]
    tabs = {}
    outs = []
    for x in (x_prompt, x_sample):
        seq = x.shape[1]
        if seq not in tabs:
            tabs[seq] = _tables(seq, rel_bias)
        outs.append(_trunk(x, layers, tabs[seq]))
    return tuple(outs)
```
